```python
import jax
import jax.numpy as jnp
from jax import lax
import numpy as np

D_MODEL = 4096
BATCH = 4
SEQ = 4096
DEPTH = 4

GRID_W = 64
CTX_LEN = 256
N_MIXERS = 3
NORM_EPS = 1e-6
ADA_RANK = 512
N_MOD = 6

HG_HEADS = 32
HG_DK = D_MODEL // HG_HEADS
HG_DV = D_MODEL // HG_HEADS
HG_CHUNK = 64

SSD_D_INNER = 2 * D_MODEL
SSD_HEADDIM = 64
SSD_HEADS = SSD_D_INNER // SSD_HEADDIM
SSD_GROUPS = 8
SSD_HEADS_PER_GROUP = SSD_HEADS // SSD_GROUPS
SSD_STATE = 128
SSD_CONV = 5
SSD_CHUNK = 128
SSD_CONV_DIM = SSD_D_INNER + 2 * SSD_GROUPS * SSD_STATE
SSD_IN_DIM = SSD_D_INNER + SSD_CONV_DIM + 2 * SSD_HEADS

ATT_HEADS = 32
ATT_KV_HEADS = 8
ATT_GROUP = ATT_HEADS // ATT_KV_HEADS
ATT_HEAD_DIM = D_MODEL // ATT_HEADS
ATT_Q_DIM = ATT_HEADS * ATT_HEAD_DIM
ATT_KV_DIM = ATT_KV_HEADS * ATT_HEAD_DIM
ATT_WINDOW = 128
ATT_BLOCK = 128
ROPE_THETA = 10000.0

N_EXPERTS = 32
EXPERT_FF = 256
SHARED_FF = 1024
TOP_K = 8
N_EXPERT_GROUPS = 8
TOPK_GROUPS = 4
ROUTED_SCALE = 2.5

N_HG_LAYERS = (DEPTH + 2) // 3
N_SSD_LAYERS = (DEPTH + 1) // 3
N_ATT_LAYERS = DEPTH // 3

kernel_name = 'hybrid_hgrn2_ssd_swa_moe_dit'


def rms_norm(x, w):
    xf = x.astype(jnp.float32)
    y = xf * lax.rsqrt(jnp.mean(xf * xf, axis=-1, keepdims=True) + NORM_EPS)
    return (y * w.astype(jnp.float32)).astype(x.dtype)


def ada_modulation(cond, w_down, w_up, bias):
    m = (jax.nn.silu(cond) @ w_down) @ w_up + bias
    return jnp.split(m, N_MOD, axis=-1)


def modulate(x, w, shift, scale):
    return rms_norm(x, w) * (1 + scale) + shift


def flip_seq(t):
    return jnp.flip(t, axis=1)


def centred_depthwise_conv(x, w, b):
    pad = w.shape[0] // 2
    y = lax.conv_general_dilated(x, w[:, None, :].astype(x.dtype), window_strides=(1,),
                                 padding=((pad, pad),), dimension_numbers=('NWC', 'WIO', 'NWC'),
                                 feature_group_count=x.shape[-1])
    return y + b.astype(x.dtype)


def hgrn2_gates(f_raw, lb):
    f_raw = f_raw.astype(jnp.float32)
    log_f = jnp.logaddexp(jnp.log(lb), jnp.log1p(-lb) + jax.nn.log_sigmoid(f_raw))
    one_minus_f = (1.0 - lb) * jax.nn.sigmoid(-f_raw)
    return log_f, one_minus_f


def gla_scan(q, k, v, log_f, s0):
    bsz, length, heads, _ = q.shape
    dv = v.shape[-1]
    n = length // HG_CHUNK

    def to_chunks(t):
        return t.reshape(bsz, n, HG_CHUNK, heads, t.shape[-1]).transpose(1, 0, 3, 2, 4)

    causal = jnp.tril(jnp.ones((HG_CHUNK, HG_CHUNK), bool))[:, :, None]

    def step(s, inp):
        qi, ki, vi, gi = inp
        cum = jnp.cumsum(gi, axis=2)
        diff = cum[:, :, :, None, :] - cum[:, :, None, :, :]
        decay = jnp.exp(jnp.where(causal, diff, -jnp.inf))
        scores = jnp.einsum('bhtd,bhsd,bhtsd->bhts', qi, ki, decay)
        o = jnp.einsum('bhts,bhsv->bhtv', scores, vi) + jnp.einsum('bhtd,bhdv->bhtv', qi * jnp.exp(cum), s)
        last = cum[:, :, -1:, :]
        s_new = jnp.exp(last[:, :, 0, :])[..., None] * s + jnp.einsum('bhsd,bhsv->bhdv', ki * jnp.exp(last - cum), vi)
        return s_new, o

    s_fin, o = lax.scan(step, s0, (to_chunks(q), to_chunks(k), to_chunks(v), to_chunks(log_f)))
    return o.transpose(1, 0, 3, 2, 4).reshape(bsz, length, heads, dv), s_fin


def hgrn2_mixer(h_lat, h_ctx, w_in, lb, norm_w, w_out, need_ctx_out):
    def prep(h):
        bsz, n, _ = h.shape
        q, f_fw, f_bw, i_in, g = jnp.split(h @ w_in, 5, axis=-1)
        lf_fw, k_fw = hgrn2_gates(f_fw, lb[0])
        lf_bw, k_bw = hgrn2_gates(f_bw, lb[1])

        def heads(t):
            return t.reshape(bsz, n, HG_HEADS, -1).astype(jnp.float32)

        return (heads(jax.nn.silu(q)), heads(i_in), heads(lf_fw), heads(k_fw), heads(lf_bw), heads(k_bw), g)

    def bidir(q, i_in, lf_fw, k_fw, lf_bw, k_bw, s_fw, s_bw):
        o_fw, s_fw = gla_scan(q, k_fw, i_in, lf_fw, s_fw)
        o_bw, s_bw = gla_scan(flip_seq(q), flip_seq(k_bw), flip_seq(i_in), flip_seq(lf_bw), s_bw)
        return o_fw + flip_seq(o_bw), s_fw, s_bw

    def readout(o, g):
        o = rms_norm(o, norm_w).reshape(g.shape)
        return (o * jax.nn.silu(g.astype(jnp.float32))).astype(g.dtype) @ w_out

    qc, ic, lfc_f, kc_f, lfc_b, kc_b, gc = prep(h_ctx)
    ql, il, lfl_f, kl_f, lfl_b, kl_b, gl = prep(h_lat)
    s0 = jnp.zeros((h_lat.shape[0], HG_HEADS, HG_DK, HG_DV), jnp.float32)
    o_c, s_fw, s_bw = bidir(qc, ic, lfc_f, kc_f, lfc_b, kc_b, s0, s0)
    o_l, _, _ = bidir(ql, il, lfl_f, kl_f, lfl_b, kl_b, s_fw, s_bw)
    y_lat = readout(o_l, gl)
    y_ctx = readout(o_c, gc) if need_ctx_out else None
    return y_lat, y_ctx


def ssd_scan(xs, dt, a, bm, cm, h0):
    bsz, length = xs.shape[:2]
    nc = length // SSD_CHUNK

    def chunks(t):
        return t.reshape((bsz, nc, SSD_CHUNK) + t.shape[2:]).swapaxes(0, 1)

    causal = jnp.tril(jnp.ones((SSD_CHUNK, SSD_CHUNK), bool))[None, :, :, None, None]

    def step(hs, inp):
        xi, dti, bi, ci = inp
        cum = jnp.cumsum(dti * a, axis=1)
        seg = cum[:, :, None] - cum[:, None, :]
        decay = jnp.exp(jnp.where(causal, seg, -jnp.inf))
        cb = jnp.einsum('btgn,bsgn->btsg', ci, bi)
        w = cb[..., None] * decay * dti[:, None]
        y = jnp.einsum('btsgr,bsgrp->btgrp', w, xi)
        y = y + jnp.einsum('btgn,bgrpn->btgrp', ci, hs) * jnp.exp(cum)[..., None]
        to_end = jnp.exp(cum[:, -1:] - cum) * dti
        hs = jnp.exp(cum[:, -1])[..., None, None] * hs + jnp.einsum('bsgr,bsgn,bsgrp->bgrpn', to_end, bi, xi)
        return hs, y

    h_fin, y = lax.scan(step, h0, (chunks(xs), chunks(dt), chunks(bm), chunks(cm)))
    return y.swapaxes(0, 1).reshape(xs.shape), h_fin


def ssd_mixer(h_lat, h_ctx, w_in, conv_w, conv_b, dt_bias, a_log, d_skip, norm_w, w_out, need_ctx_out):
    f32 = jnp.float32
    a = -jnp.exp(a_log.astype(f32)).reshape(2, SSD_GROUPS, SSD_HEADS_PER_GROUP)
    dt_b = dt_bias.astype(f32)

    def prep(h):
        bsz, n, _ = h.shape
        z, xbc, dt_fw, dt_bw = jnp.split(
            h @ w_in, [SSD_D_INNER, SSD_D_INNER + SSD_CONV_DIM, SSD_D_INNER + SSD_CONV_DIM + SSD_HEADS], axis=-1)
        xbc = jax.nn.silu(centred_depthwise_conv(xbc, conv_w, conv_b))
        xs, bm, cm = jnp.split(xbc.astype(f32), [SSD_D_INNER, SSD_D_INNER + SSD_GROUPS * SSD_STATE], axis=-1)
        xs = xs.reshape(bsz, n, SSD_GROUPS, SSD_HEADS_PER_GROUP, SSD_HEADDIM)
        bm = bm.reshape(bsz, n, SSD_GROUPS, SSD_STATE)
        cm = cm.reshape(bsz, n, SSD_GROUPS, SSD_STATE)
        dt_f = jax.nn.softplus(dt_fw.astype(f32) + dt_b[0]).reshape(bsz, n, SSD_GROUPS, SSD_HEADS_PER_GROUP)
        dt_r = jax.nn.softplus(dt_bw.astype(f32) + dt_b[1]).reshape(bsz, n, SSD_GROUPS, SSD_HEADS_PER_GROUP)
        return z, xs, bm, cm, dt_f, dt_r

    def bidir(xs, bm, cm, dt_f, dt_r, h_fw, h_bw):
        y_fw, h_fw = ssd_scan(xs, dt_f, a[0], bm, cm, h_fw)
        y_bw, h_bw = ssd_scan(flip_seq(xs), flip_seq(dt_r), a[1], flip_seq(bm), flip_seq(cm), h_bw)
        return y_fw + flip_seq(y_bw), h_fw, h_bw

    def readout(y, xs, z):
        bsz, n, _ = z.shape
        y = (y + d_skip.astype(f32).reshape(SSD_GROUPS, SSD_HEADS_PER_GROUP, 1) * xs).reshape(bsz, n, SSD_D_INNER)
        y = y * jax.nn.silu(z.astype(f32))
        y = rms_norm(y.reshape(bsz, n, SSD_GROUPS, -1), norm_w.reshape(SSD_GROUPS, -1)).reshape(bsz, n, SSD_D_INNER)
        return y.astype(z.dtype) @ w_out

    zc, xc, bc, cc, dfc, dbc = prep(h_ctx)
    zl, xl, bl, cl, dfl, dbl = prep(h_lat)
    h0 = jnp.zeros((h_lat.shape[0], SSD_GROUPS, SSD_HEADS_PER_GROUP, SSD_HEADDIM, SSD_STATE), f32)
    y_c, h_fw, h_bw = bidir(xc, bc, cc, dfc, dbc, h0, h0)
    y_l, _, _ = bidir(xl, bl, cl, dfl, dbl, h_fw, h_bw)
    y_lat = readout(y_l, xl, zl)
    y_ctx = readout(y_c, xc, zc) if need_ctx_out else None
    return y_lat, y_ctx


def axial_rope(length):
    rows = length // GRID_W
    row = jnp.repeat(jnp.arange(rows, dtype=jnp.float32), GRID_W)
    col = (jnp.arange(length) % GRID_W).astype(jnp.float32)
    n_freq = ATT_HEAD_DIM // 4
    inv_freq = ROPE_THETA ** (-jnp.arange(n_freq, dtype=jnp.float32) / n_freq)
    ang = jnp.concatenate([row[:, None] * inv_freq, col[:, None] * inv_freq], axis=-1)
    return jnp.cos(ang), jnp.sin(ang)


def apply_rope(t, cos, sin):
    shape = (1, cos.shape[0]) + (1,) * (t.ndim - 3) + (cos.shape[1],)
    cos, sin = cos.reshape(shape), sin.reshape(shape)
    t1, t2 = jnp.split(t.astype(jnp.float32), 2, axis=-1)
    return jnp.concatenate([t1 * cos - t2 * sin, t1 * sin + t2 * cos], axis=-1).astype(t.dtype)


def sink_softmax_attend(q, keys, values, mask, sink_logit):
    s = jnp.einsum('bqkgd,bskd->bkgqs', q, keys).astype(jnp.float32)
    if mask is not None:
        s = jnp.where(mask, s, -jnp.inf)
    sink_col = jnp.broadcast_to(sink_logit[None, :, :, None, None], s.shape[:-1] + (1,))
    p = jax.nn.softmax(jnp.concatenate([s, sink_col], axis=-1), axis=-1)[..., :-1]
    return jnp.einsum('bkgqs,bskd->bqkgd', p.astype(values.dtype), values)


def window_attention_mixer(h_lat, h_ctx, w_in, sink, w_out, cos, sin, need_ctx_out):
    bsz, length, _ = h_lat.shape
    ctx_len = h_ctx.shape[1]
    scale = ATT_HEAD_DIM ** -0.5

    def prep(h):
        n = h.shape[1]
        q, k, v = jnp.split(h @ w_in, [ATT_Q_DIM, ATT_Q_DIM + ATT_KV_DIM], axis=-1)
        return (q.reshape(bsz, n, ATT_KV_HEADS, ATT_GROUP, ATT_HEAD_DIM) * scale,
                k.reshape(bsz, n, ATT_KV_HEADS, ATT_HEAD_DIM), v.reshape(bsz, n, ATT_KV_HEADS, ATT_HEAD_DIM))

    ql, kl, vl = prep(h_lat)
    qc, kc, vc = prep(h_ctx)
    ql, kl = apply_rope(ql, cos, sin), apply_rope(kl, cos, sin)
    sink_logit = sink.reshape(ATT_KV_HEADS, ATT_GROUP).astype(jnp.float32)

    nb = length // ATT_BLOCK
    pad = ((0, 0), (ATT_BLOCK, ATT_BLOCK), (0, 0), (0, 0))
    kp, vp = jnp.pad(kl, pad), jnp.pad(vl, pad)
    offs = jnp.arange(3 * ATT_BLOCK) - ATT_BLOCK
    qoff = jnp.arange(ATT_BLOCK)
    band = jnp.abs(qoff[:, None] - offs[None, :]) <= ATT_WINDOW
    ctx_cols = jnp.ones((ATT_BLOCK, ctx_len), bool)

    def block(j):
        start = j * ATT_BLOCK
        qb = lax.dynamic_slice_in_dim(ql, start, ATT_BLOCK, axis=1)
        kb = lax.dynamic_slice_in_dim(kp, start, 3 * ATT_BLOCK, axis=1)
        vb = lax.dynamic_slice_in_dim(vp, start, 3 * ATT_BLOCK, axis=1)
        kpos = start + offs
        mask = band & ((kpos >= 0) & (kpos < length))[None, :]
        return sink_softmax_attend(qb, jnp.concatenate([kb, kc], axis=1), jnp.concatenate([vb, vc], axis=1),
                                   jnp.concatenate([mask, ctx_cols], axis=1), sink_logit)

    o = lax.map(block, jnp.arange(nb))
    y_lat = o.swapaxes(0, 1).reshape(bsz, length, ATT_Q_DIM) @ w_out
    y_ctx = None
    if need_ctx_out:
        y_ctx = sink_softmax_attend(qc, kc, vc, None, sink_logit).reshape(bsz, ctx_len, ATT_Q_DIM) @ w_out
    return y_lat, y_ctx


def moe_ffn(h, router_w, router_bias, w_gate, w_up, w_down, sw_gate, sw_up, sw_down):
    t = h.shape[0]
    scores = jax.nn.sigmoid((h @ router_w).astype(jnp.float32))
    biased = scores + router_bias.astype(jnp.float32)
    group_score = lax.top_k(biased.reshape(t, N_EXPERT_GROUPS, -1), 2)[0].sum(-1)
    _, top_groups = lax.top_k(group_score, TOPK_GROUPS)
    group_mask = jnp.any(top_groups[:, :, None] == jnp.arange(N_EXPERT_GROUPS), axis=1)
    expert_mask = jnp.repeat(group_mask, N_EXPERTS // N_EXPERT_GROUPS, axis=1)
    _, top_idx = lax.top_k(jnp.where(expert_mask, biased, -jnp.inf), TOP_K)
    w = jnp.take_along_axis(scores, top_idx, axis=1)
    w = ROUTED_SCALE * w / jnp.sum(w, axis=1, keepdims=True)
    gates = jnp.sum(jax.nn.one_hot(top_idx, N_EXPERTS, dtype=jnp.float32) * w[..., None], axis=1).astype(h.dtype)
    hg = jnp.einsum('td,edf->tef', h, w_gate)
    hu = jnp.einsum('td,edf->tef', h, w_up)
    routed = jnp.einsum('tef,efd->td', jax.nn.silu(hg) * hu * gates[..., None], w_down)
    shared = (jax.nn.silu(h @ sw_gate) * (h @ sw_up)) @ sw_down
    return routed + shared


def setup_inputs(seed: int = 0) -> dict:
    key = jax.random.key(seed)
    keys = iter(jax.random.split(key, 48))
    f32 = jnp.float32

    def normal(shape, scale=1.0):
        return scale * jax.random.normal(next(keys), shape, f32)

    def dense(shape, fan_in, gain=1.0):
        return normal(shape, gain * fan_in ** -0.5)

    def norm_gain(shape):
        return 1.0 + normal(shape, 0.05)

    dt = jnp.exp(jax.random.uniform(next(keys), (N_SSD_LAYERS, 2, SSD_HEADS), f32, np.log(1e-3), np.log(1e-1)))
    ssd_dt_bias = dt + jnp.log(-jnp.expm1(-dt))
    ssd_a_log = jnp.log(jax.random.uniform(next(keys), (N_SSD_LAYERS, 2, SSD_HEADS), f32, 1.0, 16.0))
    return {
        'x': normal((BATCH, SEQ, D_MODEL)),
        'c': normal((BATCH, D_MODEL)),
        'ctx': normal((BATCH, CTX_LEN, D_MODEL)),
        'c_ctx': normal((D_MODEL,)),
        'ada_w_down': dense((DEPTH, D_MODEL, ADA_RANK), D_MODEL),
        'ada_w_up': dense((DEPTH, ADA_RANK, N_MOD * D_MODEL), ADA_RANK, 0.2),
        'ada_b': normal((DEPTH, N_MOD * D_MODEL), 0.02),
        'norm_mix': norm_gain((DEPTH, D_MODEL)),
        'norm_ffn': norm_gain((DEPTH, D_MODEL)),
        'final_norm': norm_gain((D_MODEL,)),
        'hg_w_in': dense((N_HG_LAYERS, D_MODEL, 5 * D_MODEL), D_MODEL),
        'hg_lb_logits': normal((2, DEPTH, D_MODEL), 0.5),
        'hg_norm': norm_gain((N_HG_LAYERS, HG_DV)),
        'hg_w_out': dense((N_HG_LAYERS, D_MODEL, D_MODEL), D_MODEL),
        'ssd_w_in': dense((N_SSD_LAYERS, D_MODEL, SSD_IN_DIM), D_MODEL),
        'ssd_conv_w': dense((N_SSD_LAYERS, SSD_CONV, SSD_CONV_DIM), SSD_CONV),
        'ssd_conv_b': normal((N_SSD_LAYERS, SSD_CONV_DIM), 0.02),
        'ssd_dt_bias': ssd_dt_bias,
        'ssd_a_log': ssd_a_log,
        'ssd_d': 1.0 + normal((N_SSD_LAYERS, SSD_HEADS), 0.05),
        'ssd_norm': norm_gain((N_SSD_LAYERS, SSD_D_INNER)),
        'ssd_w_out': dense((N_SSD_LAYERS, SSD_D_INNER, D_MODEL), SSD_D_INNER),
        'att_w_in': dense((N_ATT_LAYERS, D_MODEL, ATT_Q_DIM + 2 * ATT_KV_DIM), D_MODEL),
        'att_sink': normal((N_ATT_LAYERS, ATT_HEADS), 0.5),
        'att_w_out': dense((N_ATT_LAYERS, ATT_Q_DIM, D_MODEL), ATT_Q_DIM),
        'moe_router': dense((DEPTH, D_MODEL, N_EXPERTS), D_MODEL),
        'moe_bias': normal((DEPTH, N_EXPERTS), 0.01),
        'moe_w_gate': dense((DEPTH, N_EXPERTS, D_MODEL, EXPERT_FF), D_MODEL),
        'moe_w_up': dense((DEPTH, N_EXPERTS, D_MODEL, EXPERT_FF), D_MODEL),
        'moe_w_down': dense((DEPTH, N_EXPERTS, EXPERT_FF, D_MODEL), EXPERT_FF),
        'moe_sw_gate': dense((DEPTH, D_MODEL, SHARED_FF), D_MODEL),
        'moe_sw_up': dense((DEPTH, D_MODEL, SHARED_FF), D_MODEL),
        'moe_sw_down': dense((DEPTH, SHARED_FF, D_MODEL), SHARED_FF),
    }


def reference(x, c, ctx, c_ctx, ada_w_down, ada_w_up, ada_b, norm_mix, norm_ffn, final_norm,
              hg_w_in, hg_lb_logits, hg_norm, hg_w_out,
              ssd_w_in, ssd_conv_w, ssd_conv_b, ssd_dt_bias, ssd_a_log, ssd_d, ssd_norm, ssd_w_out,
              att_w_in, att_sink, att_w_out,
              moe_router, moe_bias, moe_w_gate, moe_w_up, moe_w_down, moe_sw_gate, moe_sw_up, moe_sw_down):
    bsz, length, d = x.shape
    ctx_len = ctx.shape[1]
    cos, sin = axial_rope(length)
    lb_p = jax.nn.softmax(hg_lb_logits.astype(jnp.float32), axis=1)
    lower_bounds = jnp.cumsum(lb_p, axis=1) - lb_p[:, :1]
    n_hg = n_ssd = n_att = 0
    for i in range(DEPTH):
        last = i == DEPTH - 1
        sh_m, sc_m, g_m, sh_f, sc_f, g_f = [m[:, None, :] for m in ada_modulation(c, ada_w_down[i], ada_w_up[i], ada_b[i])]
        csh_m, csc_m, cg_m, csh_f, csc_f, cg_f = ada_modulation(c_ctx, ada_w_down[i], ada_w_up[i], ada_b[i])
        h_lat = modulate(x, norm_mix[i], sh_m, sc_m)
        h_ctx = modulate(ctx, norm_mix[i], csh_m, csc_m)
        kind = i % N_MIXERS
        if kind == 0:
            y_lat, y_ctx = hgrn2_mixer(h_lat, h_ctx, hg_w_in[n_hg], lower_bounds[:, i], hg_norm[n_hg],
                                       hg_w_out[n_hg], not last)
            n_hg += 1
        elif kind == 1:
            y_lat, y_ctx = ssd_mixer(h_lat, h_ctx, ssd_w_in[n_ssd], ssd_conv_w[n_ssd], ssd_conv_b[n_ssd],
                                     ssd_dt_bias[n_ssd], ssd_a_log[n_ssd], ssd_d[n_ssd], ssd_norm[n_ssd],
                                     ssd_w_out[n_ssd], not last)
            n_ssd += 1
        else:
            y_lat, y_ctx = window_attention_mixer(h_lat, h_ctx, att_w_in[n_att], att_sink[n_att], att_w_out[n_att],
                                                  cos, sin, not last)
            n_att += 1
        x = x + g_m * y_lat
        h_lat = modulate(x, norm_ffn[i], sh_f, sc_f).reshape(bsz * length, d)
        moe_args = (moe_router[i], moe_bias[i], moe_w_gate[i], moe_w_up[i], moe_w_down[i],
                    moe_sw_gate[i], moe_sw_up[i], moe_sw_down[i])
        if last:
            x = x + g_f * moe_ffn(h_lat, *moe_args).reshape(bsz, length, d)
        else:
            ctx = ctx + cg_m * y_ctx
            h_ctx = modulate(ctx, norm_ffn[i], csh_f, csc_f).reshape(bsz * ctx_len, d)
            y = moe_ffn(jnp.concatenate([h_lat, h_ctx], axis=0), *moe_args)
            x = x + g_f * y[: bsz * length].reshape(bsz, length, d)
            ctx = ctx + cg_f * y[bsz * length:].reshape(bsz, ctx_len, d)
    return rms_norm(x, final_norm)
```

```python
import functools

import jax
import jax.numpy as jnp
from jax import lax
from jax.experimental import pallas as pl
from jax.experimental.pallas import tpu as pltpu

F32 = jnp.float32
BF16 = jnp.bfloat16

D_MODEL = 4096
BATCH = 4
SEQ = 4096
DEPTH = 4
GRID_W = 64
CTX_LEN = 256
NORM_EPS = 1e-6
N_MOD = 6

HG_HEADS = 32
HG_DK = 128
HG_CHUNK = 64
HG_SUB = 16

SSD_D_INNER = 2 * D_MODEL
SSD_HEADDIM = 64
SSD_HEADS = SSD_D_INNER // SSD_HEADDIM
SSD_GROUPS = 8
SSD_HPG = SSD_HEADS // SSD_GROUPS
SSD_STATE = 128
SSD_CONV = 5
SSD_CHUNK = 128
SSD_GROUP_W = SSD_HPG * SSD_HEADDIM

ATT_HEADS = 32
ATT_KV_HEADS = 8
ATT_GROUP = ATT_HEADS // ATT_KV_HEADS
ATT_HEAD_DIM = 128
ATT_BLOCK = 128
ROPE_THETA = 10000.0

N_EXPERTS = 32
EXPERT_FF = 256
SHARED_FF = 1024
TOP_K = 8
N_EXPERT_GROUPS = 8
EXPERTS_PER_GROUP = N_EXPERTS // N_EXPERT_GROUPS
TOPK_GROUPS = 4
ROUTED_SCALE = 2.5

LANES = 128
VMEM_LIMIT_BYTES = 56 * 1024 * 1024
EXP_CLAMP = 80.0


def _params(*sem):
    return pltpu.CompilerParams(dimension_semantics=sem, vmem_limit_bytes=VMEM_LIMIT_BYTES)


def _row_group(i, tm):
    return jnp.minimum((i * tm) // SEQ, BATCH)


def _split3(x):
    hi = x.astype(BF16)
    r1 = x - hi.astype(F32)
    mid = r1.astype(BF16)
    lo = (r1 - mid.astype(F32)).astype(BF16)
    return hi, mid, lo


def _sigmoid(x):
    return 1.0 / (1.0 + jnp.exp(-x))


def _silu(x):
    return x * _sigmoid(x)


def _mm_kernel(a_ref, w_ref, o_ref):
    o_ref[...] = jnp.dot(a_ref[...], w_ref[...], preferred_element_type=F32).astype(o_ref.dtype)


def matmul(a, w, *, tm, tn, out_dtype=F32, name="mm"):
    m, k = a.shape
    n = w.shape[1]
    return pl.pallas_call(
        _mm_kernel,
        grid=(m // tm, n // tn),
        in_specs=[pl.BlockSpec((tm, k), lambda i, j: (i, 0)),
                  pl.BlockSpec((k, tn), lambda i, j: (0, j))],
        out_specs=pl.BlockSpec((tm, tn), lambda i, j: (i, j)),
        out_shape=jax.ShapeDtypeStruct((m, n), out_dtype),
        compiler_params=_params("arbitrary", "arbitrary"),
        name=name,
    )(a, w)


def _mm_resid_kernel(a_ref, w_ref, x_ref, g_ref, o_ref):
    y = jnp.dot(a_ref[...], w_ref[...], preferred_element_type=F32)
    o_ref[...] = x_ref[...] + g_ref[0] * y


def matmul_residual(a, w, x, gate, *, tm, tn, name="mm_resid"):
    m, k = a.shape
    n = w.shape[1]
    return pl.pallas_call(
        _mm_resid_kernel,
        grid=(m // tm, n // tn),
        in_specs=[pl.BlockSpec((tm, k), lambda i, j: (i, 0)),
                  pl.BlockSpec((k, tn), lambda i, j: (0, j)),
                  pl.BlockSpec((tm, tn), lambda i, j: (i, j)),
                  pl.BlockSpec((1, 1, tn), lambda i, j: (_row_group(i, tm), 0, j))],
        out_specs=pl.BlockSpec((tm, tn), lambda i, j: (i, j)),
        out_shape=jax.ShapeDtypeStruct((m, n), F32),
        compiler_params=_params("arbitrary", "arbitrary"),
        name=name,
    )(a, w, x, gate)


def _mm_glu_kernel(a_ref, wg_ref, wu_ref, gates_ref, o_ref):
    e = pl.program_id(1)
    a = a_ref[...]
    hg = jnp.dot(a, wg_ref[...], preferred_element_type=F32)
    hu = jnp.dot(a, wu_ref[...], preferred_element_type=F32)
    gates = gates_ref[...]
    lane = lax.broadcasted_iota(jnp.int32, gates.shape, 1)
    gcol = jnp.sum(jnp.where(lane == e, gates, 0.0), axis=-1, keepdims=True)
    o_ref[...] = (_silu(hg) * hu * gcol).astype(o_ref.dtype)


def matmul_glu(a, wg, wu, gates, *, tm, name="moe_up"):
    m, k = a.shape
    ne, _, f = wg.shape
    return pl.pallas_call(
        _mm_glu_kernel,
        grid=(m // tm, ne),
        in_specs=[pl.BlockSpec((tm, k), lambda i, e: (i, 0)),
                  pl.BlockSpec((None, k, f), lambda i, e: (e, 0, 0)),
                  pl.BlockSpec((None, k, f), lambda i, e: (e, 0, 0)),
                  pl.BlockSpec((tm, LANES), lambda i, e: (i, 0))],
        out_specs=pl.BlockSpec((tm, f), lambda i, e: (i, e)),
        out_shape=jax.ShapeDtypeStruct((m, ne * f), BF16),
        compiler_params=_params("arbitrary", "arbitrary"),
        name=name,
    )(a, wg, wu, gates)


def _modulate_kernel(x_ref, w_ref, sh_ref, sc_ref, o_ref):
    x = x_ref[...]
    ms = jnp.mean(x * x, axis=-1, keepdims=True)
    y = x * lax.rsqrt(ms + NORM_EPS) * w_ref[...]
    o_ref[...] = (y * (1.0 + sc_ref[0]) + sh_ref[0]).astype(o_ref.dtype)


def modulate(x, w, shift, scale, *, tm=256, out_dtype=BF16, name="modulate"):
    m, d = x.shape
    return pl.pallas_call(
        _modulate_kernel,
        grid=(m // tm,),
        in_specs=[pl.BlockSpec((tm, d), lambda i: (i, 0)),
                  pl.BlockSpec((1, d), lambda i: (0, 0)),
                  pl.BlockSpec((1, 1, d), lambda i: (_row_group(i, tm), 0, 0)),
                  pl.BlockSpec((1, 1, d), lambda i: (_row_group(i, tm), 0, 0))],
        out_specs=pl.BlockSpec((tm, d), lambda i: (i, 0)),
        out_shape=jax.ShapeDtypeStruct((m, d), out_dtype),
        compiler_params=_params("arbitrary"),
        name=name,
    )(x, w.reshape(1, d), shift, scale)


def _chunk_block(b, c, *, chunk, reverse):
    nl = SEQ // chunk
    nc = CTX_LEN // chunk
    lat_blocks = BATCH * nl
    if reverse:
        cc = jnp.where(c < nc, nc - 1 - c, nl - 1 - (c - nc))
    else:
        cc = jnp.where(c < nc, c, c - nc)
    return jnp.where(c < nc, lat_blocks + b * nc + cc, b * nl + cc)


def _order_matrix(n, reverse):
    row = lax.broadcasted_iota(jnp.int32, (n, n), 0)
    col = lax.broadcasted_iota(jnp.int32, (n, n), 1)
    return (col >= row) if reverse else (col <= row)


def _exact_cumsum(order_bf16, x):
    hi, mid, lo = _split3(x)
    dot = functools.partial(jnp.dot, preferred_element_type=F32)
    return dot(order_bf16, hi) + dot(order_bf16, mid) + dot(order_bf16, lo)


def _rows(x, start, n):
    return jnp.broadcast_to(x[start:start + 1, :], (n, x.shape[1]))


def _hg_level_refs(cum, reverse):
    zero = jnp.zeros((HG_SUB, cum.shape[1]), F32)
    if not reverse:
        ref1 = _rows(cum, 31, HG_CHUNK)
        ref2 = jnp.concatenate([_rows(cum, 15, 32), _rows(cum, 47, 32)], axis=0)
        ref3 = jnp.concatenate([zero, _rows(cum, 15, 16), _rows(cum, 31, 16), _rows(cum, 47, 16)], axis=0)
    else:
        ref1 = _rows(cum, 32, HG_CHUNK)
        ref2 = jnp.concatenate([_rows(cum, 16, 32), _rows(cum, 48, 32)], axis=0)
        ref3 = jnp.concatenate([_rows(cum, 16, 16), _rows(cum, 32, 16), _rows(cum, 48, 16), zero], axis=0)
    return ref1, ref2, ref3


def _hg_level_masks(reverse):
    n = HG_CHUNK
    t = lax.broadcasted_iota(jnp.int32, (n, n), 0)
    s = lax.broadcasted_iota(jnp.int32, (n, n), 1)
    if reverse:
        t, s = s, t
    m1 = (t >= 32) & (s < 32)
    same_half = (t >= 32) == (s >= 32)
    t_hi_q = (t & 31) >= 16
    s_lo_q = (s & 31) < 16
    m2 = same_half & t_hi_q & s_lo_q
    m3 = ((t >> 4) == (s >> 4)) & (s <= t)
    return m1, m2, m3


def _hg_scan_kernel(*refs, reverse, heads, readout):
    if readout:
        (q_ref, f_ref, i_ref, loglb_ref, l1mlb_ref, omlb_ref, ofw_ref, g_ref, nw_ref, o_ref, st_ref) = refs
    else:
        (q_ref, f_ref, i_ref, loglb_ref, l1mlb_ref, omlb_ref, o_ref, st_ref) = refs
    c = pl.program_id(2)

    @pl.when(c == 0)
    def _():
        st_ref[...] = jnp.zeros_like(st_ref)

    order = _order_matrix(HG_CHUNK, reverse)
    order_bf16 = order.astype(BF16)
    m1, m2, m3 = _hg_level_masks(reverse)
    end_row = 0 if reverse else HG_CHUNK - 1
    nt = (((1,), (1,)), ((), ()))
    tn = (((0,), (0,)), ((), ()))

    for h in range(heads):
        sl = slice(h * HG_DK, (h + 1) * HG_DK)
        fr = f_ref[:, sl]
        e = jnp.exp(-jnp.abs(fr))
        log_sig = jnp.minimum(fr, 0.0) - jnp.log1p(e)
        a_ = loglb_ref[:, sl]
        b_ = l1mlb_ref[:, sl] + log_sig
        lf = jnp.maximum(a_, b_) + jnp.log1p(jnp.exp(-jnp.abs(a_ - b_)))
        k = omlb_ref[:, sl] * (jnp.where(fr >= 0.0, e, 1.0) / (1.0 + e))
        qs = _silu(q_ref[:, sl])
        v = i_ref[:, sl].astype(BF16)

        cum = _exact_cumsum(order_bf16, lf)
        ref1, ref2, ref3 = _hg_level_refs(cum, reverse)

        def level(ref, mask, clamp_k):
            qq = (qs * jnp.exp(jnp.minimum(cum - ref, 0.0))).astype(BF16)
            kk = (k * jnp.exp(jnp.minimum(ref - cum, clamp_k))).astype(BF16)
            sc = lax.dot_general(qq, kk, nt, preferred_element_type=F32)
            return jnp.where(mask, sc, 0.0)

        scores = level(ref1, m1, 0.0) + level(ref2, m2, 0.0) + level(ref3, m3, EXP_CLAMP)
        o = jnp.dot(scores.astype(BF16), v, preferred_element_type=F32)

        st = st_ref[h]
        q_in = (qs * jnp.exp(cum)).astype(BF16)
        o = o + lax.dot_general(q_in, st.astype(BF16), nt, preferred_element_type=F32)

        cum_end = cum[end_row:end_row + 1, :]
        k_out = (k * jnp.exp(cum_end - cum)).astype(BF16)
        st_ref[h] = st * jnp.exp(cum_end) + lax.dot_general(v, k_out, tn, preferred_element_type=F32)

        if readout:
            o = o + ofw_ref[:, sl]
            ms = jnp.mean(o * o, axis=-1, keepdims=True)
            y = o * lax.rsqrt(ms + NORM_EPS) * nw_ref[...]
            o_ref[:, sl] = (y * _silu(g_ref[:, sl])).astype(o_ref.dtype)
        else:
            o_ref[:, sl] = o


def hgrn2_scan(proj, loglb, l1mlb, omlb, *, reverse, o_fw=None, norm_w=None, heads_per_step=2):
    t_rows, _ = proj.shape
    d = D_MODEL
    hw = heads_per_step * HG_DK
    ncb = d // hw
    steps = (SEQ + CTX_LEN) // HG_CHUNK
    readout = o_fw is not None

    def rows(b, hg, c):
        return _chunk_block(b, c, chunk=HG_CHUNK, reverse=reverse)

    def sec(section):
        return pl.BlockSpec((HG_CHUNK, hw), lambda b, hg, c: (rows(b, hg, c), section * ncb + hg))

    vec = pl.BlockSpec((1, hw), lambda b, hg, c: (0, hg))
    in_specs = [sec(0), sec(2 if reverse else 1), sec(3), vec, vec, vec]
    args = [proj, proj, proj, loglb, l1mlb, omlb]
    out_spec = pl.BlockSpec((HG_CHUNK, hw), lambda b, hg, c: (rows(b, hg, c), hg))
    if readout:
        in_specs += [out_spec, sec(4), pl.BlockSpec((1, HG_DK), lambda b, hg, c: (0, 0))]
        args += [o_fw, proj, norm_w.reshape(1, HG_DK)]
    return pl.pallas_call(
        functools.partial(_hg_scan_kernel, reverse=reverse, heads=heads_per_step, readout=readout),
        grid=(BATCH, ncb, steps),
        in_specs=in_specs,
        out_specs=out_spec,
        out_shape=jax.ShapeDtypeStruct((t_rows, d), BF16 if readout else F32),
        scratch_shapes=[pltpu.VMEM((heads_per_step, HG_DK, HG_DK), F32)],
        compiler_params=_params("arbitrary", "arbitrary", "arbitrary"),
        name="hg_scan_bw" if reverse else "hg_scan_fw",
    )(*args)


CONV_ROWS = 256
CONV_HALO = 8


def _conv_kernel(prev_ref, x_ref, next_ref, w_ref, b_ref, o_ref):
    i = pl.program_id(0)
    lat_blocks = (BATCH * SEQ) // CONV_ROWS
    per_seq = SEQ // CONV_ROWS
    is_ctx = i >= lat_blocks
    first = is_ctx | ((i % per_seq) == 0)
    last = is_ctx | ((i % per_seq) == per_seq - 1)
    prev = jnp.where(first, 0.0, prev_ref[...])
    nxt = jnp.where(last, 0.0, next_ref[...])
    ext = jnp.concatenate([prev, x_ref[...], nxt], axis=0)
    pad = SSD_CONV // 2
    acc = b_ref[...] + jnp.zeros(x_ref.shape, F32)
    for k in range(SSD_CONV):
        start = CONV_HALO - pad + k
        acc = acc + w_ref[k:k + 1, :] * ext[start:start + CONV_ROWS, :]
    o_ref[...] = _silu(acc).astype(o_ref.dtype)


def ssd_conv(proj, conv_w, conv_b, *, col0, width, tw=512):
    t_rows = proj.shape[0]
    nblk = t_rows // CONV_ROWS
    halo_per_blk = CONV_ROWS // CONV_HALO
    n_halo = t_rows // CONV_HALO
    c0 = col0 // tw
    kw = conv_w.shape[0]
    return pl.pallas_call(
        _conv_kernel,
        grid=(nblk, width // tw),
        in_specs=[pl.BlockSpec((CONV_HALO, tw), lambda i, j: (jnp.maximum(i * halo_per_blk - 1, 0), c0 + j)),
                  pl.BlockSpec((CONV_ROWS, tw), lambda i, j: (i, c0 + j)),
                  pl.BlockSpec((CONV_HALO, tw), lambda i, j: (jnp.minimum((i + 1) * halo_per_blk, n_halo - 1), c0 + j)),
                  pl.BlockSpec((kw, tw), lambda i, j: (0, j)),
                  pl.BlockSpec((1, tw), lambda i, j: (0, j))],
        out_specs=pl.BlockSpec((CONV_ROWS, tw), lambda i, j: (i, j)),
        out_shape=jax.ShapeDtypeStruct((t_rows, width), BF16),
        compiler_params=_params("arbitrary", "arbitrary"),
        name="ssd_conv",
    )(proj, proj, proj, conv_w, conv_b.reshape(1, width))


def _softplus(x):
    return jnp.maximum(x, 0.0) + jnp.log1p(jnp.exp(-jnp.abs(x)))


def _ssd_scan_kernel(*refs, reverse, readout):
    if readout:
        (x_ref, b_ref, c_ref, dtc_ref, dtr_ref, arow_ref, acol_ref, brow_ref, bcol_ref,
         yfw_ref, z_ref, dsk_ref, nw_ref, o_ref, st_ref) = refs
    else:
        (x_ref, b_ref, c_ref, dtc_ref, dtr_ref, arow_ref, acol_ref, brow_ref, bcol_ref,
         o_ref, st_ref) = refs
    c = pl.program_id(2)

    @pl.when(c == 0)
    def _():
        st_ref[...] = jnp.zeros_like(st_ref)

    n = SSD_CHUNK
    order = _order_matrix(n, reverse)
    order_bf16 = order.astype(BF16)
    nt = (((1,), (1,)), ((), ()))
    dot = functools.partial(jnp.dot, preferred_element_type=F32)

    dt_c = _softplus(dtc_ref[...] + brow_ref[...])
    dt_r = _softplus(dtr_ref[...] + bcol_ref[...])
    cum_c = _exact_cumsum(order_bf16, dt_c * arow_ref[...])
    hi, mid, lo = _split3(dt_r * acol_ref[...])
    cum_r = (lax.dot_general(hi, order_bf16, nt, preferred_element_type=F32)
             + lax.dot_general(mid, order_bf16, nt, preferred_element_type=F32)
             + lax.dot_general(lo, order_bf16, nt, preferred_element_type=F32))
    end_row = 0 if reverse else n - 1
    cum_end = cum_c[end_row:end_row + 1, :]
    to_end = jnp.exp(cum_end - cum_c) * dt_c
    from_start = jnp.exp(cum_c)
    st_decay = jnp.exp(cum_end)

    bm = b_ref[...]
    cm = c_ref[...]
    cb = lax.dot_general(cm, bm, nt, preferred_element_type=F32)
    bm_t = bm.astype(F32).T.astype(BF16)

    lane = lax.broadcasted_iota(jnp.int32, (n, 2 * SSD_HEADDIM), 1)
    left = lane < SSD_HEADDIM
    left1 = left[0:1, :]
    gated = []

    for p in range(SSD_HPG // 2):
        r0, r1 = 2 * p, 2 * p + 1
        psl = slice(p * 2 * SSD_HEADDIM, (p + 1) * 2 * SSD_HEADDIM)
        xp = x_ref[:, psl]
        ws = []
        for r in (r0, r1):
            seg = cum_c[:, r:r + 1] - cum_r[r:r + 1, :]
            dec = jnp.exp(jnp.where(order, seg, -jnp.inf))
            ws.append((cb * dec * dt_r[r:r + 1, :]).astype(BF16))
        w2 = jnp.concatenate(ws, axis=1)
        zero = jnp.zeros_like(xp)
        x2 = jnp.concatenate([jnp.where(left, xp, zero), jnp.where(left, zero, xp)], axis=0)
        y = dot(w2, x2)

        st = st_ref[:, psl]
        scale_in = jnp.where(left, from_start[:, r0:r0 + 1], from_start[:, r1:r1 + 1])
        y = y + dot(cm, st.astype(BF16)) * scale_in

        xs = (xp.astype(F32) * jnp.where(left, to_end[:, r0:r0 + 1], to_end[:, r1:r1 + 1])).astype(BF16)
        keep = jnp.where(left1, st_decay[:, r0:r0 + 1], st_decay[:, r1:r1 + 1])
        st_ref[:, psl] = st * keep + dot(bm_t, xs)

        if readout:
            y = y + yfw_ref[:, psl] + dsk_ref[:, psl] * xp.astype(F32)
            gated.append(y * _silu(z_ref[:, psl]))
        else:
            o_ref[:, psl] = y

    if readout:
        y = jnp.concatenate(gated, axis=1)
        ms = jnp.mean(y * y, axis=-1, keepdims=True)
        o_ref[...] = (y * lax.rsqrt(ms + NORM_EPS) * nw_ref[...]).astype(o_ref.dtype)


def ssd_scan(proj, xbc, dt_cols, dt_rows, a_row, a_col, bias_row, bias_col, *, reverse,
             y_fw=None, d_skip=None, norm_w=None):
    t_rows = xbc.shape[0]
    steps = (SEQ + CTX_LEN) // SSD_CHUNK
    gw = SSD_GROUP_W
    nxb = SSD_D_INNER // SSD_STATE
    readout = y_fw is not None

    def rows(b, c):
        return _chunk_block(b, c, chunk=SSD_CHUNK, reverse=reverse)

    small = lambda shape: pl.BlockSpec((None,) + shape, lambda b, g, c: (g, 0, 0))
    out_spec = pl.BlockSpec((SSD_CHUNK, gw), lambda b, g, c: (rows(b, c), g))
    in_specs = [out_spec,
                pl.BlockSpec((SSD_CHUNK, SSD_STATE), lambda b, g, c: (rows(b, c), nxb + g)),
                pl.BlockSpec((SSD_CHUNK, SSD_STATE), lambda b, g, c: (rows(b, c), nxb + SSD_GROUPS + g)),
                pl.BlockSpec((None, SSD_CHUNK, LANES), lambda b, g, c: (g, rows(b, c), 0)),
                pl.BlockSpec((None, SSD_HPG, SSD_CHUNK), lambda b, g, c: (g, 0, rows(b, c))),
                small((1, LANES)), small((SSD_HPG, 1)), small((1, LANES)), small((SSD_HPG, 1))]
    args = [xbc, xbc, xbc, dt_cols, dt_rows, a_row, a_col, bias_row, bias_col]
    if readout:
        vec = pl.BlockSpec((1, gw), lambda b, g, c: (0, g))
        in_specs += [out_spec, out_spec, vec, vec]
        args += [y_fw, proj, d_skip, norm_w]
    return pl.pallas_call(
        functools.partial(_ssd_scan_kernel, reverse=reverse, readout=readout),
        grid=(BATCH, SSD_GROUPS, steps),
        in_specs=in_specs,
        out_specs=out_spec,
        out_shape=jax.ShapeDtypeStruct((t_rows, SSD_D_INNER), BF16 if readout else F32),
        scratch_shapes=[pltpu.VMEM((SSD_STATE, gw), F32)],
        compiler_params=_params("arbitrary", "arbitrary", "arbitrary"),
        name="ssd_scan_bw" if reverse else "ssd_scan_fw",
    )(*args)


def _rope_kernel(x_ref, cos_ref, sin_ref, o_ref, *, heads, scale):
    cos = cos_ref[...]
    sin = sin_ref[...]
    for h in range(heads):
        sl = slice(h * ATT_HEAD_DIM, (h + 1) * ATT_HEAD_DIM)
        t = x_ref[:, sl]
        swapped = pltpu.roll(t, ATT_HEAD_DIM // 2, 1)
        o_ref[:, sl] = ((t * cos + swapped * sin) * scale).astype(o_ref.dtype)


def rope(proj, cos, sin, *, col0, heads, scale, tm=256):
    t_rows = proj.shape[0]
    w = heads * ATT_HEAD_DIM
    c0 = col0 // w
    return pl.pallas_call(
        functools.partial(_rope_kernel, heads=heads, scale=scale),
        grid=(t_rows // tm,),
        in_specs=[pl.BlockSpec((tm, w), lambda i: (i, c0)),
                  pl.BlockSpec((tm, ATT_HEAD_DIM), lambda i: (i, 0)),
                  pl.BlockSpec((tm, ATT_HEAD_DIM), lambda i: (i, 0))],
        out_specs=pl.BlockSpec((tm, w), lambda i: (i, 0)),
        out_shape=jax.ShapeDtypeStruct((t_rows, w), BF16),
        compiler_params=_params("arbitrary"),
        name="rope",
    )(proj, cos, sin)


def _attn_kernel(q_ref, kp_ref, kc_ref, kn_ref, kx_ref, vp_ref, vc_ref, vn_ref, vx_ref, sink_ref, o_ref):
    j = pl.program_id(2)
    nb = SEQ // ATT_BLOCK
    blk = ATT_BLOCK
    keys = jnp.concatenate([kp_ref[...], kc_ref[...], kn_ref[...], kx_ref[...]], axis=0)
    vals = jnp.concatenate([vp_ref[...].astype(BF16), vc_ref[...].astype(BF16),
                            vn_ref[...].astype(BF16), vx_ref[...].astype(BF16)], axis=0)
    nk = keys.shape[0]
    row = lax.broadcasted_iota(jnp.int32, (blk, nk), 0)
    col = lax.broadcasted_iota(jnp.int32, (blk, nk), 1)
    is_lat = j < nb
    in_prev = (col < blk) & (j > 0) & (col >= row)
    in_cur = (col >= blk) & (col < 2 * blk)
    in_next = (col >= 2 * blk) & (col < 3 * blk) & (j < nb - 1) & (col - 2 * blk <= row)
    mask = (col >= 3 * blk) | (is_lat & (in_prev | in_cur | in_next))
    nt = (((1,), (1,)), ((), ()))
    for h in range(ATT_GROUP):
        sl = slice(h * ATT_HEAD_DIM, (h + 1) * ATT_HEAD_DIM)
        s = lax.dot_general(q_ref[:, sl], keys, nt, preferred_element_type=F32)
        s = jnp.where(mask, s, -jnp.inf)
        sink = sink_ref[:, h:h + 1]
        m = jnp.maximum(jnp.max(s, axis=-1, keepdims=True), sink)
        p = jnp.exp(s - m)
        denom = jnp.sum(p, axis=-1, keepdims=True) + jnp.exp(sink - m)
        o = jnp.dot(p.astype(BF16), vals, preferred_element_type=F32)
        o_ref[:, sl] = (o / denom).astype(o_ref.dtype)


def window_attention(q, k, proj, sink, *, v_col0):
    t_rows = q.shape[0]
    nb = SEQ // ATT_BLOCK
    ncb = CTX_LEN // ATT_BLOCK
    lat_blocks = BATCH * nb
    gw = ATT_GROUP * ATT_HEAD_DIM
    vc0 = v_col0 // ATT_HEAD_DIM

    def qrow(b, j):
        return jnp.where(j < nb, b * nb + jnp.minimum(j, nb - 1), lat_blocks + b * ncb + (j - nb))

    def krow(off):
        return lambda b, j: b * nb + jnp.clip(j + off, 0, nb - 1)

    def kspec(off, col):
        fn = krow(off)
        return pl.BlockSpec((ATT_BLOCK, ATT_HEAD_DIM), lambda b, kv, j: (fn(b, j), col + kv))

    def xspec(col):
        return pl.BlockSpec((CTX_LEN, ATT_HEAD_DIM),
                            lambda b, kv, j: ((BATCH * SEQ) // CTX_LEN + b, col + kv))

    qspec = pl.BlockSpec((ATT_BLOCK, gw), lambda b, kv, j: (qrow(b, j), kv))
    return pl.pallas_call(
        _attn_kernel,
        grid=(BATCH, ATT_KV_HEADS, nb + ncb),
        in_specs=[qspec, kspec(-1, 0), kspec(0, 0), kspec(1, 0), xspec(0),
                  kspec(-1, vc0), kspec(0, vc0), kspec(1, vc0), xspec(vc0),
                  pl.BlockSpec((None, 1, LANES), lambda b, kv, j: (kv, 0, 0))],
        out_specs=qspec,
        out_shape=jax.ShapeDtypeStruct((t_rows, ATT_HEADS * ATT_HEAD_DIM), BF16),
        compiler_params=_params("arbitrary", "arbitrary", "arbitrary"),
        name="window_attn",
    )(q, k, k, k, k, proj, proj, proj, proj, sink)


def _router_kernel(h_ref, whi_ref, wlo_ref, bias_ref, o_ref):
    nt = (((1,), (1,)), ((), ()))
    h = h_ref[...]
    logits = (lax.dot_general(whi_ref[...], h, nt, preferred_element_type=F32)
              + lax.dot_general(wlo_ref[...], h, nt, preferred_element_type=F32))
    scores = _sigmoid(logits)
    biased = scores + bias_ref[...]
    ng = N_EXPERT_GROUPS
    npg = EXPERTS_PER_GROUP
    sc = [scores[m * ng:(m + 1) * ng, :] for m in range(npg)]
    bi = [biased[m * ng:(m + 1) * ng, :] for m in range(npg)]
    tok = h.shape[0]
    gscore = None
    for m0 in range(npg):
        for m1 in range(m0 + 1, npg):
            pair = bi[m0] + bi[m1]
            gscore = pair if gscore is None else jnp.maximum(gscore, pair)
    gidx = lax.broadcasted_iota(jnp.int32, (ng, tok), 0)
    beaten = jnp.zeros((ng, tok), jnp.int32)
    for g in range(ng):
        other = jnp.broadcast_to(gscore[g:g + 1, :], (ng, tok))
        wins = (other > gscore) | ((other == gscore) & (g < gidx))
        beaten = beaten + wins.astype(jnp.int32)
    gmask = beaten < TOPK_GROUPS
    masked = [jnp.where(gmask, bi[m], -jnp.inf) for m in range(npg)]
    eidx = [gidx * npg + m for m in range(npg)]
    rank = [jnp.zeros((ng, tok), jnp.int32) for _ in range(npg)]
    for m1 in range(npg):
        for g in range(ng):
            other = jnp.broadcast_to(masked[m1][g:g + 1, :], (ng, tok))
            oidx = g * npg + m1
            for m in range(npg):
                wins = (other > masked[m]) | ((other == masked[m]) & (oidx < eidx[m]))
                rank[m] = rank[m] + wins.astype(jnp.int32)
    sel = [jnp.where(rank[m] < TOP_K, sc[m], 0.0) for m in range(npg)]
    total = sel[0]
    for m in range(1, npg):
        total = total + sel[m]
    denom = jnp.sum(total, axis=0, keepdims=True)
    for m in range(npg):
        o_ref[m * ng:(m + 1) * ng, :] = ROUTED_SCALE * sel[m] / denom


def router_gates(h, router_w, router_bias, *, tm):
    t_rows, d = h.shape
    ne = N_EXPERTS
    ng = N_EXPERT_GROUPS
    npg = EXPERTS_PER_GROUP
    perm = jnp.arange(ne).reshape(ng, npg).T.reshape(ne)
    w_t = router_w.T[perm]
    w_hi = w_t.astype(BF16)
    w_lo = (w_t - w_hi.astype(F32)).astype(BF16)
    bias = router_bias.astype(F32)[perm].reshape(ne, 1)
    gates_t = pl.pallas_call(
        _router_kernel,
        grid=(t_rows // tm,),
        in_specs=[pl.BlockSpec((tm, d), lambda i: (i, 0)),
                  pl.BlockSpec((ne, d), lambda i: (0, 0)),
                  pl.BlockSpec((ne, d), lambda i: (0, 0)),
                  pl.BlockSpec((ne, 1), lambda i: (0, 0))],
        out_specs=pl.BlockSpec((ne, tm), lambda i: (0, i)),
        out_shape=jax.ShapeDtypeStruct((ne, t_rows), F32),
        compiler_params=_params("arbitrary"),
        name="moe_router",
    )(h, w_hi, w_lo, bias)
    inv = jnp.argsort(perm)
    gates = gates_t[inv].T
    n_shared = SHARED_FF // EXPERT_FF
    return jnp.concatenate([gates, jnp.ones((t_rows, n_shared), F32),
                            jnp.zeros((t_rows, LANES - ne - n_shared), F32)], axis=1)


TM = 1024


def ada_vectors(cond, w_down, w_up, bias):
    g = cond.shape[0]
    pad = 16 - g
    a = jnp.pad(_silu(cond), ((0, pad), (0, 0))).astype(BF16)
    low = matmul(a, w_down.astype(BF16), tm=16, tn=w_down.shape[1], name="ada_down")
    m = matmul(low.astype(BF16), w_up.astype(BF16), tm=16, tn=2048, name="ada_up")[:g] + bias
    return [v.reshape(g, 1, D_MODEL) for v in jnp.split(m, N_MOD, axis=-1)]


def hgrn2_layer(h, w_in, lb, norm_w):
    proj = matmul(h, w_in.astype(BF16), tm=TM, tn=512, name="hg_in")
    loglb = jnp.log(lb)
    l1mlb = jnp.log1p(-lb)
    omlb = 1.0 - lb
    o_fw = hgrn2_scan(proj, loglb[0:1], l1mlb[0:1], omlb[0:1], reverse=False)
    return hgrn2_scan(proj, loglb[1:2], l1mlb[1:2], omlb[1:2], reverse=True, o_fw=o_fw, norm_w=norm_w)


def ssd_layer(h, w_in, conv_w, conv_b, dt_bias, a_log, d_skip, norm_w):
    t_rows = h.shape[0]
    n_in = w_in.shape[1]
    tn = 512
    n_pad = -n_in % tn
    w = jnp.pad(w_in, ((0, 0), (0, n_pad))).astype(BF16)
    proj = matmul(h, w, tm=TM, tn=tn, name="ssd_in")
    conv_dim = SSD_D_INNER + 2 * SSD_GROUPS * SSD_STATE
    xbc = ssd_conv(proj, conv_w, conv_b, col0=SSD_D_INNER, width=conv_dim)
    dt0 = SSD_D_INNER + conv_dim
    a = -jnp.exp(a_log.astype(F32)).reshape(2, SSD_GROUPS, SSD_HPG)
    bias = dt_bias.astype(F32).reshape(2, SSD_GROUPS, SSD_HPG)
    lane_pad = ((0, 0), (0, 0), (0, LANES - SSD_HPG))
    ys = None
    for direction in (0, 1):
        raw = proj[:, dt0 + direction * SSD_HEADS: dt0 + (direction + 1) * SSD_HEADS]
        raw = raw.reshape(t_rows, SSD_GROUPS, SSD_HPG).transpose(1, 0, 2)
        dt_cols = jnp.pad(raw, lane_pad)
        dt_rows = raw.transpose(0, 2, 1)
        a_d, b_d = a[direction], bias[direction]
        a_row = jnp.pad(a_d, ((0, 0), (0, LANES - SSD_HPG))).reshape(SSD_GROUPS, 1, LANES)
        b_row = jnp.pad(b_d, ((0, 0), (0, LANES - SSD_HPG))).reshape(SSD_GROUPS, 1, LANES)
        a_col = a_d.reshape(SSD_GROUPS, SSD_HPG, 1)
        b_col = b_d.reshape(SSD_GROUPS, SSD_HPG, 1)
        if direction == 0:
            ys = ssd_scan(proj, xbc, dt_cols, dt_rows, a_row, a_col, b_row, b_col, reverse=False)
        else:
            d_lanes = jnp.repeat(d_skip.astype(F32), SSD_HEADDIM).reshape(1, SSD_D_INNER)
            ys = ssd_scan(proj, xbc, dt_cols, dt_rows, a_row, a_col, b_row, b_col, reverse=True,
                          y_fw=ys, d_skip=d_lanes, norm_w=norm_w.astype(F32).reshape(1, SSD_D_INNER))
    return ys


def rope_tables():
    rows = SEQ // GRID_W
    row = jnp.repeat(jnp.arange(rows, dtype=F32), GRID_W)
    col = (jnp.arange(SEQ) % GRID_W).astype(F32)
    n_freq = ATT_HEAD_DIM // 4
    inv_freq = ROPE_THETA ** (-jnp.arange(n_freq, dtype=F32) / n_freq)
    ang = jnp.concatenate([row[:, None] * inv_freq, col[:, None] * inv_freq], axis=-1)
    cos = jnp.concatenate([jnp.cos(ang), jnp.cos(ang)], axis=-1)
    sin = jnp.concatenate([-jnp.sin(ang), jnp.sin(ang)], axis=-1)
    n_ctx = BATCH * CTX_LEN
    cos = jnp.concatenate([jnp.tile(cos, (BATCH, 1)), jnp.ones((n_ctx, ATT_HEAD_DIM), F32)], axis=0)
    sin = jnp.concatenate([jnp.tile(sin, (BATCH, 1)), jnp.zeros((n_ctx, ATT_HEAD_DIM), F32)], axis=0)
    return cos, sin


def attention_layer(h, w_in, sink, cos, sin):
    proj = matmul(h, w_in.astype(BF16), tm=TM, tn=512, name="att_in")
    q_dim = ATT_HEADS * ATT_HEAD_DIM
    kv_dim = ATT_KV_HEADS * ATT_HEAD_DIM
    q = rope(proj, cos, sin, col0=0, heads=ATT_HEADS, scale=ATT_HEAD_DIM ** -0.5)
    k = rope(proj, cos, sin, col0=q_dim, heads=ATT_KV_HEADS, scale=1.0)
    sink_g = jnp.pad(sink.astype(F32).reshape(ATT_KV_HEADS, 1, ATT_GROUP),
                     ((0, 0), (0, 0), (0, LANES - ATT_GROUP)))
    return window_attention(q, k, proj, sink_g, v_col0=q_dim + kv_dim)


def moe_layer(x, h, gate_f, router_w, router_bias, w_gate, w_up, w_down, sw_gate, sw_up, sw_down):
    d = D_MODEL
    n_shared = SHARED_FF // EXPERT_FF
    gates = router_gates(h, router_w, router_bias, tm=TM)

    def stack(w_e, w_s):
        shared = w_s.reshape(d, n_shared, EXPERT_FF).transpose(1, 0, 2)
        return jnp.concatenate([w_e.astype(BF16), shared.astype(BF16)], axis=0)

    act = matmul_glu(h, stack(w_gate, sw_gate), stack(w_up, sw_up), gates, tm=TM)
    w_dn = jnp.concatenate([w_down.reshape(N_EXPERTS * EXPERT_FF, d).astype(BF16), sw_down.astype(BF16)], axis=0)
    return matmul_residual(act, w_dn, x, gate_f, tm=512, tn=512, name="moe_down")


def kernel(x, c, ctx, c_ctx, ada_w_down, ada_w_up, ada_b, norm_mix, norm_ffn, final_norm, hg_w_in, hg_lb_logits, hg_norm, hg_w_out, ssd_w_in, ssd_conv_w, ssd_conv_b, ssd_dt_bias, ssd_a_log, ssd_d, ssd_norm, ssd_w_out, att_w_in, att_sink, att_w_out, moe_router, moe_bias, moe_w_gate, moe_w_up, moe_w_down, moe_sw_gate, moe_sw_up, moe_sw_down):
    bsz, length, d = x.shape
    n_lat = bsz * length
    xs = jnp.concatenate([x.reshape(n_lat, d), ctx.reshape(-1, d)], axis=0)
    cond = jnp.concatenate([c, c_ctx[None, :]], axis=0)
    cos, sin = rope_tables()
    lb_p = jax.nn.softmax(hg_lb_logits.astype(F32), axis=1)
    lower_bounds = jnp.concatenate([jnp.zeros_like(lb_p[:, :1]), jnp.cumsum(lb_p[:, 1:], axis=1)], axis=1)
    n_hg = n_ssd = n_att = 0
    for i in range(DEPTH):
        last = i == DEPTH - 1
        sh_m, sc_m, g_m, sh_f, sc_f, g_f = ada_vectors(cond, ada_w_down[i], ada_w_up[i], ada_b[i])
        h = modulate(xs, norm_mix[i], sh_m, sc_m)
        kind = i % 3
        if kind == 0:
            a = hgrn2_layer(h, hg_w_in[n_hg], lower_bounds[:, i], hg_norm[n_hg])
            w_out = hg_w_out[n_hg]
            n_hg += 1
        elif kind == 1:
            a = ssd_layer(h, ssd_w_in[n_ssd], ssd_conv_w[n_ssd], ssd_conv_b[n_ssd], ssd_dt_bias[n_ssd],
                          ssd_a_log[n_ssd], ssd_d[n_ssd], ssd_norm[n_ssd])
            w_out = ssd_w_out[n_ssd]
            n_ssd += 1
        else:
            a = attention_layer(h, att_w_in[n_att], att_sink[n_att], cos, sin)
            w_out = att_w_out[n_att]
            n_att += 1
        if last:
            xs, a = xs[:n_lat], a[:n_lat]
        xs = matmul_residual(a, w_out.astype(BF16), xs, g_m, tm=512, tn=512, name="mix_out")
        h = modulate(xs, norm_ffn[i], sh_f, sc_f)
        xs = moe_layer(xs, h, g_f, moe_router[i], moe_bias[i], moe_w_gate[i], moe_w_up[i], moe_w_down[i],
                       moe_sw_gate[i], moe_sw_up[i], moe_sw_down[i])
    zeros = jnp.zeros((BATCH + 1, 1, d), F32)
    out = modulate(xs, final_norm, zeros, zeros, out_dtype=F32, name="final_norm")
    return out.reshape(bsz, length, d)
```
